```python
import math
import jax, jax.numpy as jnp
from jax import lax
import numpy as np

D_MODEL = 1024
BATCH = 1
SEQ = 16384
DEPTH = 4

CHUNK = 64
QBLK = 128
POOL_WIDTH = D_MODEL
POOL_GROUPS = 4
POOL_GROUP = POOL_WIDTH // POOL_GROUPS
POOL_WINDOWS = (2, 4, 8, 16)
DIFF_QK_DIM = 128
DIFF_V_DIM = 2 * DIFF_QK_DIM
DIFF_HEADS = D_MODEL // DIFF_V_DIM
DIFF_QK_WIDTH = DIFF_HEADS * 2 * DIFF_QK_DIM
DIFF_V_WIDTH = DIFF_HEADS * DIFF_V_DIM
ROT_DIM = DIFF_QK_DIM // 4
ROPE_THETA = 500000.0
N_BRANCHES = 2
IN_WIDTH = POOL_WIDTH + 2 * DIFF_QK_WIDTH + DIFF_V_WIDTH + N_BRANCHES * D_MODEL
FFN_HIDDEN = -(-8 * D_MODEL // (3 * 256)) * 256
NORM_EPS = 1e-6

kernel_name = "hybrid_pool_diffattn_gated_block"


def rmsnorm(x, g):
    xf = x.astype(jnp.float32)
    y = xf * lax.rsqrt(jnp.mean(xf * xf, axis=-1, keepdims=True) + NORM_EPS)
    return (y * g.astype(jnp.float32)).astype(x.dtype)


def rotary_tables(seq, dtype):
    pos = jnp.arange(seq, dtype=jnp.float32)
    inv_freq = ROPE_THETA ** (-jnp.arange(0, ROT_DIM, 2, dtype=jnp.float32) / ROT_DIM)
    ang = pos[:, None] * inv_freq[None, :]
    return jnp.cos(ang).astype(dtype), jnp.sin(ang).astype(dtype)


def apply_partial_rotary(t, cos, sin):
    half = ROT_DIM // 2
    t1, t2, rest = t[..., :half], t[..., half:ROT_DIM], t[..., ROT_DIM:]
    return jnp.concatenate([t1 * cos - t2 * sin, t2 * cos + t1 * sin, rest], axis=-1)


def pool_mixer(u, w_pool, scale):
    b, s, _ = u.shape
    ug = u.reshape(b, s, POOL_GROUPS, POOL_GROUP)
    t_idx = jnp.arange(s, dtype=jnp.float32)[None, :, None]
    outs = []
    for gi, w in enumerate(POOL_WINDOWS):
        xg = ug[:, :, gi, :].astype(jnp.float32)
        c = jnp.cumsum(xg, axis=1)
        c_shift = jnp.pad(c, ((0, 0), (w, 0), (0, 0)))[:, :s]
        count = jnp.minimum(t_idx + 1.0, float(w))
        outs.append((c - c_shift) / count - xg)
    pooled = jnp.stack(outs, axis=2).astype(u.dtype)
    y = jnp.einsum('bsgc,gcd->bsgd', pooled, w_pool)
    return y.reshape(b, s, POOL_WIDTH) * scale


def diff_attention(q, k, v, lam):
    b, h, _, s, d = q.shape
    nblk = s // QBLK
    chunk_ids = jnp.arange(s) // CHUNK
    qb = q.reshape(b, h, 2, nblk, QBLK, d).transpose(3, 0, 1, 2, 4, 5)
    qc = chunk_ids.reshape(nblk, QBLK)
    scale = DIFF_QK_DIM ** -0.5
    neg = jnp.finfo(jnp.float32).min

    def one_block(args):
        qi, ci = args
        sc = jnp.einsum('bhcqd,bhckd->bhcqk', qi, k).astype(jnp.float32) * scale
        mask = ci[:, None] >= chunk_ids[None, :]
        p = jax.nn.softmax(jnp.where(mask, sc, neg), axis=-1)
        a = p[:, :, 0] - lam * p[:, :, 1]
        return jnp.einsum('bhqk,bhkv->bhqv', a.astype(v.dtype), v)

    out = lax.map(one_block, (qb, qc))
    return out.transpose(1, 2, 0, 3, 4).reshape(b, h, s, DIFF_V_DIM)


def setup_inputs(seed: int = 0) -> dict:
    key = jax.random.key(seed)
    ks = jax.random.split(key, 16)
    f32 = jnp.float32
    nrm = lambda k, shp, std: jax.random.normal(k, shp, f32) * std
    return {
        "x": jax.random.normal(ks[0], (BATCH, SEQ, D_MODEL), f32),
        "norm1_g": 1.0 + nrm(ks[1], (DEPTH, D_MODEL), 0.02),
        "w_in": nrm(ks[2], (DEPTH, D_MODEL, IN_WIDTH), D_MODEL ** -0.5),
        "q_norm_g": 1.0 + nrm(ks[3], (DEPTH, DIFF_QK_DIM), 0.02),
        "k_norm_g": 1.0 + nrm(ks[4], (DEPTH, DIFF_QK_DIM), 0.02),
        "lam_q1": nrm(ks[5], (DEPTH, DIFF_QK_DIM), 0.1),
        "lam_k1": nrm(ks[6], (DEPTH, DIFF_QK_DIM), 0.1),
        "lam_q2": nrm(ks[7], (DEPTH, DIFF_QK_DIM), 0.1),
        "lam_k2": nrm(ks[8], (DEPTH, DIFF_QK_DIM), 0.1),
        "subln_g": 1.0 + nrm(ks[9], (DEPTH, DIFF_V_DIM), 0.02),
        "pool_w": nrm(ks[10], (DEPTH, POOL_GROUPS, POOL_GROUP, POOL_GROUP), POOL_GROUP ** -0.5),
        "pool_scale": 1.0 + nrm(ks[11], (DEPTH, POOL_WIDTH), 0.02),
        "w_out": nrm(ks[12], (DEPTH, D_MODEL, D_MODEL), D_MODEL ** -0.5),
        "norm2_g": 1.0 + nrm(ks[13], (DEPTH, D_MODEL), 0.02),
        "w_ffn_in": nrm(ks[14], (DEPTH, D_MODEL, 2 * FFN_HIDDEN), D_MODEL ** -0.5),
        "w_ffn_out": nrm(ks[15], (DEPTH, FFN_HIDDEN, D_MODEL), FFN_HIDDEN ** -0.5),
    }


def reference(x, norm1_g, w_in, q_norm_g, k_norm_g, lam_q1, lam_k1, lam_q2, lam_k2,
              subln_g, pool_w, pool_scale, w_out, norm2_g, w_ffn_in, w_ffn_out):
    b, s, _ = x.shape
    cos, sin = rotary_tables(s, x.dtype)
    splits = np.cumsum([POOL_WIDTH, DIFF_QK_WIDTH, DIFF_QK_WIDTH, DIFF_V_WIDTH, D_MODEL]).tolist()
    for i in range(DEPTH):
        lam_init = 0.8 - 0.6 * math.exp(-0.3 * i)
        xn = rmsnorm(x, norm1_g[i])
        proj = xn @ w_in[i]
        u_pool, q, k, v, g_a, g_b = jnp.split(proj, splits, axis=-1)
        q = q.reshape(b, s, DIFF_HEADS, 2, DIFF_QK_DIM).transpose(0, 2, 3, 1, 4)
        k = k.reshape(b, s, DIFF_HEADS, 2, DIFF_QK_DIM).transpose(0, 2, 3, 1, 4)
        q = apply_partial_rotary(rmsnorm(q, q_norm_g[i]), cos, sin)
        k = apply_partial_rotary(rmsnorm(k, k_norm_g[i]), cos, sin)
        v = v.reshape(b, s, DIFF_HEADS, DIFF_V_DIM).transpose(0, 2, 1, 3)
        lam = (jnp.exp(jnp.sum(lam_q1[i].astype(jnp.float32) * lam_k1[i].astype(jnp.float32)))
               - jnp.exp(jnp.sum(lam_q2[i].astype(jnp.float32) * lam_k2[i].astype(jnp.float32)))
               + lam_init)
        o = diff_attention(q, k, v, lam)
        o = rmsnorm(o, subln_g[i]) * (1.0 - lam_init)
        attn_out = o.transpose(0, 2, 1, 3).reshape(b, s, DIFF_V_WIDTH)
        pool_out = pool_mixer(u_pool, pool_w[i], pool_scale[i])
        merged = jax.nn.sigmoid(g_a) * pool_out + jax.nn.sigmoid(g_b) * attn_out
        x = x + merged @ w_out[i]
        hn = rmsnorm(x, norm2_g[i])
        gate, up = jnp.split(hn @ w_ffn_in[i], 2, axis=-1)
        x = x + (jax.nn.silu(gate) * up) @ w_ffn_out[i]
    return x
```

```python
import functools
import math

import jax
import jax.numpy as jnp
from jax import lax
from jax.experimental import pallas as pl
from jax.experimental.pallas import tpu as pltpu

CHUNK = 64
POOL_WINDOWS = (2, 4, 8, 16)
QK_DIM = 128
V_DIM = 2 * QK_DIM
ROT_DIM = QK_DIM // 4
ROPE_THETA = 500000.0
NORM_EPS = 1e-6
MASK_VALUE = -1e30

V7X_LANES = 128
V7X_VMEM_BYTES = 64 * 1024 * 1024
V7X_VMEM_RESERVE_BYTES = 8 * 1024 * 1024

ROW_TILE = 512
ATTN_TILE = 512
POOL_HALO = max(POOL_WINDOWS)

BF16 = jnp.bfloat16
F32 = jnp.float32


def _vmem_limit(block_bytes, scratch_bytes, temp_bytes):
    need = 2 * block_bytes + scratch_bytes + temp_bytes
    return int(min(need, V7X_VMEM_BYTES - V7X_VMEM_RESERVE_BYTES))


def _nbytes(shape, dtype):
    return math.prod(shape) * jnp.dtype(dtype).itemsize


def _rms(x):
    return x * lax.rsqrt(jnp.mean(x * x, axis=-1, keepdims=True) + NORM_EPS)


def _dot(a, b):
    return jnp.dot(a, b, preferred_element_type=F32)


def _qkv_kernel(x_ref, g1_ref, wq_ref, wk_ref, wv_ref, qg_ref, kg_ref,
                cos_ref, sinlo_ref, sinhi_ref, q_ref, k_ref, v_ref, *, q_scale):
    xn = (_rms(x_ref[...]) * g1_ref[...]).astype(BF16)
    cos = cos_ref[...]
    sin_lo = sinlo_ref[...]
    sin_hi = sinhi_ref[...]
    gains = (qg_ref[...] * q_scale, kg_ref[...])

    def norm_rot(t, gain):
        y = _rms(t) * gain
        return (y * cos + pltpu.roll(y, QK_DIM - ROT_DIM // 2, 1) * sin_lo
                + pltpu.roll(y, ROT_DIM // 2, 1) * sin_hi)

    n_heads = wq_ref.shape[1] // V_DIM
    for w_ref, o_ref, gain in ((wq_ref, q_ref, gains[0]), (wk_ref, k_ref, gains[1])):
        for h in range(n_heads):
            cols = slice(h * V_DIM, (h + 1) * V_DIM)
            t = _dot(xn, w_ref[:, cols])
            for c in range(2):
                sub = slice(c * QK_DIM, (c + 1) * QK_DIM)
                o_ref[:, h * V_DIM + c * QK_DIM:h * V_DIM + (c + 1) * QK_DIM] = (
                    norm_rot(t[:, sub], gain).astype(BF16))
    for h in range(n_heads):
        cols = slice(h * V_DIM, (h + 1) * V_DIM)
        v_ref[:, cols] = _dot(xn, wv_ref[:, cols]).astype(BF16)


def _qkv_call(x, g1, w_in_b, qg, kg, tabs, layer):
    s, d = x.shape
    ts = ROW_TILE
    row = lambda t: (t, 0)
    const = lambda t: (0, 0)
    wspec = lambda j: pl.BlockSpec((None, d, d), lambda t: (layer, 0, j))
    in_specs = [
        pl.BlockSpec((ts, d), row),
        pl.BlockSpec((1, d), const),
        wspec(1), wspec(2), wspec(3),
        pl.BlockSpec((1, QK_DIM), const),
        pl.BlockSpec((1, QK_DIM), const),
        pl.BlockSpec((ts, QK_DIM), row),
        pl.BlockSpec((ts, QK_DIM), row),
        pl.BlockSpec((ts, QK_DIM), row),
    ]
    out_specs = [pl.BlockSpec((ts, d), row)] * 3
    blocks = (_nbytes((ts, d), F32) + 3 * _nbytes((d, d), BF16)
              + 3 * _nbytes((ts, QK_DIM), F32) + 3 * _nbytes((ts, d), BF16))
    temps = _nbytes((ts, d), F32) + _nbytes((ts, d), BF16) + 8 * _nbytes((ts, V_DIM), F32)
    q_scale = (QK_DIM ** -0.5) * math.log2(math.e)
    return pl.pallas_call(
        functools.partial(_qkv_kernel, q_scale=q_scale),
        grid=(s // ts,),
        in_specs=in_specs,
        out_specs=out_specs,
        out_shape=[jax.ShapeDtypeStruct((s, d), BF16)] * 3,
        compiler_params=pltpu.CompilerParams(
            dimension_semantics=("arbitrary",),
            vmem_limit_bytes=_vmem_limit(blocks, 0, temps)),
        name="qkv_proj",
    )(x, g1, w_in_b, w_in_b, w_in_b, qg, kg, *tabs)


def _attn_kernel(q_ref, k_ref, v_ref, sg_ref, lq1_ref, lk1_ref, lq2_ref, lk2_ref,
                 o_ref, acc_ref, m_ref, l_ref, *, lam_init):
    i = pl.program_id(1)
    tq = q_ref.shape[0]
    tk = tq

    m_ref[...] = jnp.full(m_ref.shape, MASK_VALUE, F32)
    l_ref[...] = jnp.zeros(l_ref.shape, F32)
    acc_ref[...] = jnp.zeros(acc_ref.shape, F32)

    nt_dims = (((1,), (1,)), ((), ()))

    def step(j, masked):
        start = pl.multiple_of(j * tk, tk)
        kb = k_ref[pl.ds(start, tk), :]
        vb = v_ref[pl.ds(start, tk), :]
        if masked:
            qc = lax.broadcasted_iota(jnp.int32, (tq, tk), 0) // CHUNK
            kc = lax.broadcasted_iota(jnp.int32, (tq, tk), 1) // CHUNK
            visible = qc >= kc
        for c in range(2):
            sub = slice(c * QK_DIM, (c + 1) * QK_DIM)
            s = lax.dot_general(q_ref[:, sub], kb[:, sub], nt_dims,
                                preferred_element_type=F32)
            if masked:
                s = jnp.where(visible, s, MASK_VALUE)
            m_prev = m_ref[c]
            m_new = jnp.maximum(m_prev, jnp.max(s, axis=-1, keepdims=True))
            alpha = jnp.exp2(m_prev - m_new)
            p = jnp.exp2(s - m_new)
            l_ref[c] = alpha * l_ref[c] + jnp.sum(p, axis=-1, keepdims=True)
            acc_ref[c] = alpha * acc_ref[c] + _dot(p.astype(BF16), vb)
            m_ref[c] = m_new

    def body(j, carry):
        step(j, masked=False)
        return carry

    lax.fori_loop(0, i, body, 0)
    step(i, masked=True)

    lam = (jnp.exp(jnp.sum(lq1_ref[...] * lk1_ref[...], axis=-1, keepdims=True))
           - jnp.exp(jnp.sum(lq2_ref[...] * lk2_ref[...], axis=-1, keepdims=True))
           + lam_init)
    o = acc_ref[0] / l_ref[0] - lam * (acc_ref[1] / l_ref[1])
    o_ref[...] = (_rms(o) * sg_ref[...] * (1.0 - lam_init)).astype(BF16)


def _attn_call(q, k, v, sg, lq1, lk1, lq2, lk2, lam_init):
    s, d = q.shape
    n_heads = d // V_DIM
    tq = ATTN_TILE
    const = lambda h, i: (0, 0)
    in_specs = [
        pl.BlockSpec((tq, V_DIM), lambda h, i: (i, h)),
        pl.BlockSpec((s, V_DIM), lambda h, i: (0, h)),
        pl.BlockSpec((s, V_DIM), lambda h, i: (0, h)),
        pl.BlockSpec((1, V_DIM), const),
        pl.BlockSpec((1, QK_DIM), const),
        pl.BlockSpec((1, QK_DIM), const),
        pl.BlockSpec((1, QK_DIM), const),
        pl.BlockSpec((1, QK_DIM), const),
    ]
    scratch = [
        pltpu.VMEM((2, tq, V_DIM), F32),
        pltpu.VMEM((2, tq, 1), F32),
        pltpu.VMEM((2, tq, 1), F32),
    ]
    blocks = 2 * _nbytes((tq, V_DIM), BF16) + 2 * _nbytes((s, V_DIM), BF16)
    scratch_bytes = _nbytes((2, tq, V_DIM), F32) + 2 * _nbytes((2, tq, V7X_LANES), F32)
    temps = 6 * _nbytes((tq, tq), F32)
    return pl.pallas_call(
        functools.partial(_attn_kernel, lam_init=lam_init),
        grid=(n_heads, s // tq),
        in_specs=in_specs,
        out_specs=pl.BlockSpec((tq, V_DIM), lambda h, i: (i, h)),
        out_shape=jax.ShapeDtypeStruct((s, d), BF16),
        scratch_shapes=scratch,
        compiler_params=pltpu.CompilerParams(
            dimension_semantics=("arbitrary", "arbitrary"),
            vmem_limit_bytes=_vmem_limit(blocks, scratch_bytes, temps)),
        name="diff_attn",
    )(q, k, v, sg, lq1, lk1, lq2, lk2)


def _mix_kernel(x_ref, a_ref, g1_ref, wu_ref, wg_ref, pw_ref, ps_ref, wo_ref,
                o_ref, ubuf_ref, merged_ref):
    t = pl.program_id(0)
    ts, d = x_ref.shape
    group = d // len(POOL_WINDOWS)

    @pl.when(t == 0)
    def _():
        ubuf_ref[0:POOL_HALO, :] = jnp.zeros((POOL_HALO, d), F32)

    x = x_ref[...]
    xn = (_rms(x) * g1_ref[...]).astype(BF16)
    ubuf_ref[POOL_HALO:, :] = _dot(xn, wu_ref[...])

    row = t * ts + lax.broadcasted_iota(jnp.int32, (ts, 1), 0)
    for gi, w in enumerate(POOL_WINDOWS):
        cols = slice(gi * group, (gi + 1) * group)
        u = ubuf_ref[POOL_HALO:, cols]
        wsum = u
        for back in range(1, w):
            wsum = wsum + ubuf_ref[POOL_HALO - back:POOL_HALO - back + ts, cols]
        count = jnp.minimum(row + 1, w).astype(F32)
        pooled = wsum / count - u
        pool_out = _dot(pooled.astype(BF16), pw_ref[gi]) * ps_ref[:, cols]
        gate_a = jax.nn.sigmoid(_dot(xn, wg_ref[:, cols]))
        gate_b = jax.nn.sigmoid(_dot(xn, wg_ref[:, d + gi * group:d + (gi + 1) * group]))
        merged_ref[:, cols] = (gate_a * pool_out
                               + gate_b * a_ref[:, cols].astype(F32)).astype(BF16)

    tail = ubuf_ref[ts:ts + POOL_HALO, :]
    ubuf_ref[0:POOL_HALO, :] = tail
    o_ref[...] = x + _dot(merged_ref[...], wo_ref[...])


def _mix_call(x, attn, g1, w_in_b, pool_w_b, pool_scale, w_out_b, layer):
    s, d = x.shape
    ts = ROW_TILE
    n_groups = len(POOL_WINDOWS)
    group = d // n_groups
    row = lambda t: (t, 0)
    const = lambda t: (0, 0)
    in_specs = [
        pl.BlockSpec((ts, d), row),
        pl.BlockSpec((ts, d), row),
        pl.BlockSpec((1, d), const),
        pl.BlockSpec((None, d, d), lambda t: (layer, 0, 0)),
        pl.BlockSpec((None, d, 2 * d), lambda t: (layer, 0, 2)),
        pl.BlockSpec((None, n_groups, group, group), lambda t: (layer, 0, 0, 0)),
        pl.BlockSpec((1, d), const),
        pl.BlockSpec((None, d, d), lambda t: (layer, 0, 0)),
    ]
    blocks = (2 * _nbytes((ts, d), F32) + _nbytes((ts, d), BF16) + 4 * _nbytes((d, d), BF16)
              + _nbytes((n_groups, group, group), BF16))
    scratch_bytes = _nbytes((ts + POOL_HALO, d), F32) + _nbytes((ts, d), BF16)
    temps = 2 * _nbytes((ts, d), F32) + 8 * _nbytes((ts, group), F32)
    return pl.pallas_call(
        _mix_kernel,
        grid=(s // ts,),
        in_specs=in_specs,
        out_specs=pl.BlockSpec((ts, d), row),
        out_shape=jax.ShapeDtypeStruct((s, d), F32),
        scratch_shapes=[pltpu.VMEM((ts + POOL_HALO, d), F32), pltpu.VMEM((ts, d), BF16)],
        compiler_params=pltpu.CompilerParams(
            dimension_semantics=("arbitrary",),
            vmem_limit_bytes=_vmem_limit(blocks, scratch_bytes, temps)),
        name="pool_gate_out",
    )(x, attn, g1, w_in_b, w_in_b, pool_w_b, pool_scale, w_out_b)


FFN_CHUNK = 256


def _ffn_kernel(x_ref, g2_ref, wgate_ref, wup_ref, wdown_ref, o_ref, act_ref):
    x = x_ref[...]
    hn = (_rms(x) * g2_ref[...]).astype(BF16)
    hidden = wgate_ref.shape[1]
    for c in range(hidden // FFN_CHUNK):
        cols = slice(c * FFN_CHUNK, (c + 1) * FFN_CHUNK)
        gate = _dot(hn, wgate_ref[:, cols])
        up = _dot(hn, wup_ref[:, cols])
        act_ref[:, cols] = (gate * jax.nn.sigmoid(gate) * up).astype(BF16)
    o_ref[...] = x + _dot(act_ref[...], wdown_ref[...])


def _ffn_call(x, g2, w_ffn_in_b, w_ffn_out_b, layer):
    s, d = x.shape
    hidden = w_ffn_out_b.shape[1]
    assert hidden % FFN_CHUNK == 0
    ts = ROW_TILE
    row = lambda t: (t, 0)
    in_specs = [
        pl.BlockSpec((ts, d), row),
        pl.BlockSpec((1, d), lambda t: (0, 0)),
        pl.BlockSpec((None, d, hidden), lambda t: (layer, 0, 0)),
        pl.BlockSpec((None, d, hidden), lambda t: (layer, 0, 1)),
        pl.BlockSpec((None, hidden, d), lambda t: (layer, 0, 0)),
    ]
    blocks = 2 * _nbytes((ts, d), F32) + 3 * _nbytes((d, hidden), BF16)
    scratch_bytes = _nbytes((ts, hidden), BF16)
    temps = _nbytes((ts, d), F32) + _nbytes((ts, d), BF16) + 8 * _nbytes((ts, FFN_CHUNK), F32)
    return pl.pallas_call(
        _ffn_kernel,
        grid=(s // ts,),
        in_specs=in_specs,
        out_specs=pl.BlockSpec((ts, d), row),
        out_shape=jax.ShapeDtypeStruct((s, d), F32),
        scratch_shapes=[pltpu.VMEM((ts, hidden), BF16)],
        compiler_params=pltpu.CompilerParams(
            dimension_semantics=("arbitrary",),
            vmem_limit_bytes=_vmem_limit(blocks, scratch_bytes, temps)),
        name="swiglu_ffn",
    )(x, g2, w_ffn_in_b, w_ffn_in_b, w_ffn_out_b)


def _rotary_tables(seq):
    half = ROT_DIM // 2
    pos = jnp.arange(seq, dtype=F32)
    inv_freq = ROPE_THETA ** (-jnp.arange(0, ROT_DIM, 2, dtype=F32) / ROT_DIM)
    ang = pos[:, None] * inv_freq[None, :]
    cos, sin = jnp.cos(ang), jnp.sin(ang)
    zeros = jnp.zeros((seq, QK_DIM - ROT_DIM), F32)
    z_half = jnp.zeros((seq, half), F32)
    cos_t = jnp.concatenate([cos, cos, jnp.ones_like(zeros)], axis=-1)
    sin_lo = jnp.concatenate([-sin, z_half, zeros], axis=-1)
    sin_hi = jnp.concatenate([z_half, sin, zeros], axis=-1)
    return cos_t, sin_lo, sin_hi


def kernel(x, norm1_g, w_in, q_norm_g, k_norm_g, lam_q1, lam_k1, lam_q2, lam_k2, subln_g,
           pool_w, pool_scale, w_out, norm2_g, w_ffn_in, w_ffn_out):
    b, s, d = x.shape
    assert b == 1 and s % ROW_TILE == 0 and s % ATTN_TILE == 0 and ATTN_TILE % CHUNK == 0
    depth = w_in.shape[0]
    tabs = _rotary_tables(s)
    w_in_b = w_in.astype(BF16)
    pool_w_b = pool_w.astype(BF16)
    w_out_b = w_out.astype(BF16)
    w_ffn_in_b = w_ffn_in.astype(BF16)
    w_ffn_out_b = w_ffn_out.astype(BF16)
    vec = lambda a, i: a[i][None, :]

    h = x[0]
    for i in range(depth):
        lam_init = 0.8 - 0.6 * math.exp(-0.3 * i)
        q, k, v = _qkv_call(h, vec(norm1_g, i), w_in_b, vec(q_norm_g, i), vec(k_norm_g, i),
                            tabs, i)
        attn = _attn_call(q, k, v, vec(subln_g, i), vec(lam_q1, i), vec(lam_k1, i),
                          vec(lam_q2, i), vec(lam_k2, i), lam_init)
        h = _mix_call(h, attn, vec(norm1_g, i), w_in_b, pool_w_b, vec(pool_scale, i),
                      w_out_b, i)
        h = _ffn_call(h, vec(norm2_g, i), w_ffn_in_b, w_ffn_out_b, i)
    return h[None]
```

```python
import functools
import math

import jax
import jax.numpy as jnp
from jax import lax
from jax.experimental import pallas as pl
from jax.experimental.pallas import tpu as pltpu

CHUNK = 64
POOL_WINDOWS = (2, 4, 8, 16)
QK_DIM = 128
V_DIM = 2 * QK_DIM
ROT_DIM = QK_DIM // 4
ROPE_THETA = 500000.0
NORM_EPS = 1e-6
MASK_VALUE = -1e30

V7X_LANES = 128
V7X_VMEM_BYTES = 64 * 1024 * 1024
V7X_VMEM_RESERVE_BYTES = 8 * 1024 * 1024

ROW_TILE = 512
ATTN_TILE = 512
POOL_HALO = max(POOL_WINDOWS)

BF16 = jnp.bfloat16
F32 = jnp.float32
NT_DIMS = (((1,), (1,)), ((), ()))


def _vmem_limit(block_bytes, scratch_bytes, temp_bytes):
    need = 2 * block_bytes + scratch_bytes + temp_bytes
    return int(min(need, V7X_VMEM_BYTES - V7X_VMEM_RESERVE_BYTES))


def _nbytes(shape, dtype):
    return math.prod(shape) * jnp.dtype(dtype).itemsize


def _rms(x):
    return x * lax.rsqrt(jnp.mean(x * x, axis=-1, keepdims=True) + NORM_EPS)


def _dot(a, b):
    return jnp.dot(a, b, preferred_element_type=F32)


def _dot_nt(a, b):
    return lax.dot_general(a, b, NT_DIMS, preferred_element_type=F32)


def _qkv_kernel(x_ref, g1_ref, wq_ref, wk_ref, wvt_ref, qg_ref, kg_ref,
                cos_ref, sinlo_ref, sinhi_ref, q_ref, k_ref, vt_ref, *, q_scale):
    xn = (_rms(x_ref[...]) * g1_ref[...]).astype(BF16)
    cos = cos_ref[...]
    sin_lo = sinlo_ref[...]
    sin_hi = sinhi_ref[...]
    gains = (qg_ref[...] * q_scale, kg_ref[...])

    def norm_rot(t, gain):
        y = _rms(t) * gain
        return (y * cos + pltpu.roll(y, QK_DIM - ROT_DIM // 2, 1) * sin_lo
                + pltpu.roll(y, ROT_DIM // 2, 1) * sin_hi)

    n_heads = wq_ref.shape[1] // V_DIM
    for w_ref, o_ref, gain in ((wq_ref, q_ref, gains[0]), (wk_ref, k_ref, gains[1])):
        for h in range(n_heads):
            cols = slice(h * V_DIM, (h + 1) * V_DIM)
            t = _dot(xn, w_ref[:, cols])
            for c in range(2):
                sub = slice(c * QK_DIM, (c + 1) * QK_DIM)
                o_ref[:, h * V_DIM + c * QK_DIM:h * V_DIM + (c + 1) * QK_DIM] = (
                    norm_rot(t[:, sub], gain).astype(BF16))
    vt_ref[...] = _dot_nt(wvt_ref[...], xn).astype(BF16)


def _qkv_call(x, g1, w_in_b, wvt_b, qg, kg, tabs, layer):
    s, d = x.shape
    ts = ROW_TILE
    row = lambda t: (t, 0)
    const = lambda t: (0, 0)
    wspec = lambda j: pl.BlockSpec((None, d, d), lambda t: (layer, 0, j))
    in_specs = [
        pl.BlockSpec((ts, d), row),
        pl.BlockSpec((1, d), const),
        wspec(1), wspec(2),
        pl.BlockSpec((None, d, d), lambda t: (layer, 0, 0)),
        pl.BlockSpec((1, QK_DIM), const),
        pl.BlockSpec((1, QK_DIM), const),
        pl.BlockSpec((ts, QK_DIM), row),
        pl.BlockSpec((ts, QK_DIM), row),
        pl.BlockSpec((ts, QK_DIM), row),
    ]
    out_specs = [pl.BlockSpec((ts, d), row), pl.BlockSpec((ts, d), row),
                 pl.BlockSpec((None, d, ts), lambda t: (t, 0, 0))]
    blocks = (_nbytes((ts, d), F32) + 3 * _nbytes((d, d), BF16)
              + 3 * _nbytes((ts, QK_DIM), F32) + 3 * _nbytes((ts, d), BF16))
    temps = 2 * _nbytes((ts, d), F32) + _nbytes((ts, d), BF16) + 8 * _nbytes((ts, V_DIM), F32)
    q_scale = (QK_DIM ** -0.5) * math.log2(math.e)
    return pl.pallas_call(
        functools.partial(_qkv_kernel, q_scale=q_scale),
        grid=(s // ts,),
        in_specs=in_specs,
        out_specs=out_specs,
        out_shape=[jax.ShapeDtypeStruct((s, d), BF16), jax.ShapeDtypeStruct((s, d), BF16),
                   jax.ShapeDtypeStruct((s // ts, d, ts), BF16)],
        compiler_params=pltpu.CompilerParams(
            dimension_semantics=("arbitrary",),
            vmem_limit_bytes=_vmem_limit(blocks, 0, temps)),
        name="qkv_proj",
    )(x, g1, w_in_b, w_in_b, wvt_b, qg, kg, *tabs)


SLAB = 16
N_PARTIAL = 4


def _attn_kernel(q_ref, k_ref, vt_ref, sg_ref, lq1_ref, lk1_ref, lq2_ref, lk2_ref,
                 o_ref, s_buf, top_buf, p_buf, alpha_buf, acc_ref, m_ref, l_ref, *, lam_init):
    i = pl.program_id(1)
    tq = q_ref.shape[0]
    tk = vt_ref.shape[2]
    n_blocks = i + 1

    m_ref[...] = jnp.full(m_ref.shape, MASK_VALUE, F32)
    l_ref[...] = jnp.zeros(l_ref.shape, F32)
    acc_ref[...] = jnp.zeros(acc_ref.shape, F32)

    def scores(blk, slot):
        start = pl.multiple_of(blk * tk, tk)
        for c in range(2):
            sub = slice(c * QK_DIM, (c + 1) * QK_DIM)
            s = _dot_nt(k_ref[pl.ds(start, tk), sub], q_ref[:, sub])
            s_buf[slot, c] = s
            top_buf[slot, :, c * tq:(c + 1) * tq] = jnp.max(s, axis=0, keepdims=True)

    n_slabs = tk // SLAB

    def softmax(slot, masked):
        if masked:
            qc = lax.broadcasted_iota(jnp.int32, (SLAB, tq), 1) // CHUNK
        for c in range(2):
            lanes = slice(c * tq, (c + 1) * tq)

            def slab(r):
                s = s_buf[slot, c, r * SLAB:(r + 1) * SLAB, :]
                if masked:
                    kc = (r * SLAB + lax.broadcasted_iota(jnp.int32, (SLAB, tq), 0)) // CHUNK
                    s = jnp.where(kc <= qc, s, MASK_VALUE)
                return s

            if masked:
                tops = [slab(r) for r in range(N_PARTIAL)]
                for r in range(N_PARTIAL, n_slabs):
                    tops[r % N_PARTIAL] = jnp.maximum(tops[r % N_PARTIAL], slab(r))
                top = jnp.max(functools.reduce(jnp.maximum, tops), axis=0, keepdims=True)
            else:
                top = top_buf[slot, :, lanes]
            m_prev = m_ref[:, lanes]
            m_new = jnp.maximum(m_prev, top)
            alpha = jnp.exp2(m_prev - m_new)
            m_rows = jnp.broadcast_to(m_new, (SLAB, tq))
            sums = [None] * N_PARTIAL
            for r in range(n_slabs):
                p = jnp.exp2(slab(r) - m_rows)
                p_buf[slot, r * SLAB:(r + 1) * SLAB, lanes] = p.astype(BF16)
                sums[r % N_PARTIAL] = p if sums[r % N_PARTIAL] is None else sums[r % N_PARTIAL] + p
            total = functools.reduce(jnp.add, sums)
            l_ref[:, lanes] = alpha * l_ref[:, lanes] + jnp.sum(total, axis=0, keepdims=True)
            m_ref[:, lanes] = m_new
            alpha_buf[slot, :, lanes] = alpha

    def accumulate(blk, slot):
        acc_ref[...] = acc_ref[...] * alpha_buf[slot] + _dot(vt_ref[blk], p_buf[slot])

    def triple(r, slot_prev, slot_cur):
        scores(r, slot_prev)
        softmax(slot_cur, masked=False)
        accumulate(jnp.where(r == 1, i, r - 2), slot_prev)

    scores(i, 0)
    softmax(0, masked=True)
    scores(0, 1)

    def run(r, count):
        for step in range(count):
            triple(r + step, step % 2, 1 - step % 2)

    def pair(t, carry):
        run(1 + 2 * t, 2)
        return carry

    lax.fori_loop(0, i // 2, pair, 0)

    @pl.when(i % 2 == 1)
    def _():
        run(i, 1)

    accumulate(jnp.maximum(i - 1, 0), i % 2)

    lam = (jnp.exp(jnp.sum(lq1_ref[...] * lk1_ref[...], axis=-1, keepdims=True))
           - jnp.exp(jnp.sum(lq2_ref[...] * lk2_ref[...], axis=-1, keepdims=True))
           + lam_init)
    acc = acc_ref[...]
    l = l_ref[...]
    ot = acc[:, :tq] / l[:, :tq] - lam * (acc[:, tq:] / l[:, tq:])
    ot = ot * lax.rsqrt(jnp.mean(ot * ot, axis=0, keepdims=True) + NORM_EPS)
    o_ref[...] = (ot.T * sg_ref[...] * (1.0 - lam_init)).astype(BF16)


def _attn_call(q, k, vt, sg, lq1, lk1, lq2, lk2, lam_init):
    s, d = q.shape
    n_heads = d // V_DIM
    tq = ATTN_TILE
    n_kv, _, tk = vt.shape
    assert tk == tq
    const = lambda h, i: (0, 0)
    in_specs = [
        pl.BlockSpec((tq, V_DIM), lambda h, i: (i, h)),
        pl.BlockSpec((s, V_DIM), lambda h, i: (0, h)),
        pl.BlockSpec((n_kv, V_DIM, tk), lambda h, i: (0, h, 0)),
        pl.BlockSpec((1, V_DIM), const),
        pl.BlockSpec((1, QK_DIM), const),
        pl.BlockSpec((1, QK_DIM), const),
        pl.BlockSpec((1, QK_DIM), const),
        pl.BlockSpec((1, QK_DIM), const),
    ]
    scratch = [
        pltpu.VMEM((2, 2, tk, tq), F32),
        pltpu.VMEM((2, 1, 2 * tq), F32),
        pltpu.VMEM((2, tk, 2 * tq), BF16),
        pltpu.VMEM((2, 1, 2 * tq), F32),
        pltpu.VMEM((V_DIM, 2 * tq), F32),
        pltpu.VMEM((1, 2 * tq), F32),
        pltpu.VMEM((1, 2 * tq), F32),
    ]
    blocks = 2 * _nbytes((tq, V_DIM), BF16) + 2 * _nbytes((s, V_DIM), BF16)
    scratch_bytes = (_nbytes((2, 2, tk, tq), F32) + _nbytes((2, tk, 2 * tq), BF16)
                     + _nbytes((V_DIM, 2 * tq), F32) + 6 * _nbytes((8, 2 * tq), F32))
    temps = 4 * _nbytes((tk, tq), F32)
    return pl.pallas_call(
        functools.partial(_attn_kernel, lam_init=lam_init),
        grid=(n_heads, s // tq),
        in_specs=in_specs,
        out_specs=pl.BlockSpec((tq, V_DIM), lambda h, i: (i, h)),
        out_shape=jax.ShapeDtypeStruct((s, d), BF16),
        scratch_shapes=scratch,
        compiler_params=pltpu.CompilerParams(
            dimension_semantics=("arbitrary", "arbitrary"),
            vmem_limit_bytes=_vmem_limit(blocks, scratch_bytes, temps)),
        name="diff_attn",
    )(q, k, vt, sg, lq1, lk1, lq2, lk2)


def _mix_kernel(x_ref, a_ref, g1_ref, wu_ref, wg_ref, pw_ref, ps_ref, wo_ref,
                o_ref, ubuf_ref, merged_ref):
    t = pl.program_id(0)
    ts, d = x_ref.shape
    group = d // len(POOL_WINDOWS)

    @pl.when(t == 0)
    def _():
        ubuf_ref[0:POOL_HALO, :] = jnp.zeros((POOL_HALO, d), F32)

    x = x_ref[...]
    xn = (_rms(x) * g1_ref[...]).astype(BF16)
    ubuf_ref[POOL_HALO:, :] = _dot(xn, wu_ref[...])

    row = t * ts + lax.broadcasted_iota(jnp.int32, (ts, 1), 0)
    for gi, w in enumerate(POOL_WINDOWS):
        cols = slice(gi * group, (gi + 1) * group)
        u = ubuf_ref[POOL_HALO:, cols]
        wsum = u
        for back in range(1, w):
            wsum = wsum + ubuf_ref[POOL_HALO - back:POOL_HALO - back + ts, cols]
        count = jnp.minimum(row + 1, w).astype(F32)
        pooled = wsum / count - u
        pool_out = _dot(pooled.astype(BF16), pw_ref[gi]) * ps_ref[:, cols]
        gate_a = jax.nn.sigmoid(_dot(xn, wg_ref[:, cols]))
        gate_b = jax.nn.sigmoid(_dot(xn, wg_ref[:, d + gi * group:d + (gi + 1) * group]))
        merged_ref[:, cols] = (gate_a * pool_out
                               + gate_b * a_ref[:, cols].astype(F32)).astype(BF16)

    tail = ubuf_ref[ts:ts + POOL_HALO, :]
    ubuf_ref[0:POOL_HALO, :] = tail
    o_ref[...] = x + _dot(merged_ref[...], wo_ref[...])


def _mix_call(x, attn, g1, w_in_b, pool_w_b, pool_scale, w_out_b, layer):
    s, d = x.shape
    ts = ROW_TILE
    n_groups = len(POOL_WINDOWS)
    group = d // n_groups
    row = lambda t: (t, 0)
    const = lambda t: (0, 0)
    in_specs = [
        pl.BlockSpec((ts, d), row),
        pl.BlockSpec((ts, d), row),
        pl.BlockSpec((1, d), const),
        pl.BlockSpec((None, d, d), lambda t: (layer, 0, 0)),
        pl.BlockSpec((None, d, 2 * d), lambda t: (layer, 0, 2)),
        pl.BlockSpec((None, n_groups, group, group), lambda t: (layer, 0, 0, 0)),
        pl.BlockSpec((1, d), const),
        pl.BlockSpec((None, d, d), lambda t: (layer, 0, 0)),
    ]
    blocks = (2 * _nbytes((ts, d), F32) + _nbytes((ts, d), BF16) + 4 * _nbytes((d, d), BF16)
              + _nbytes((n_groups, group, group), BF16))
    scratch_bytes = _nbytes((ts + POOL_HALO, d), F32) + _nbytes((ts, d), BF16)
    temps = 2 * _nbytes((ts, d), F32) + 8 * _nbytes((ts, group), F32)
    return pl.pallas_call(
        _mix_kernel,
        grid=(s // ts,),
        in_specs=in_specs,
        out_specs=pl.BlockSpec((ts, d), row),
        out_shape=jax.ShapeDtypeStruct((s, d), F32),
        scratch_shapes=[pltpu.VMEM((ts + POOL_HALO, d), F32), pltpu.VMEM((ts, d), BF16)],
        compiler_params=pltpu.CompilerParams(
            dimension_semantics=("arbitrary",),
            vmem_limit_bytes=_vmem_limit(blocks, scratch_bytes, temps)),
        name="pool_gate_out",
    )(x, attn, g1, w_in_b, w_in_b, pool_w_b, pool_scale, w_out_b)


FFN_CHUNK = 256


def _ffn_kernel(x_ref, g2_ref, wgate_ref, wup_ref, wdown_ref, o_ref, act_ref):
    x = x_ref[...]
    hn = (_rms(x) * g2_ref[...]).astype(BF16)
    hidden = wgate_ref.shape[1]
    for c in range(hidden // FFN_CHUNK):
        cols = slice(c * FFN_CHUNK, (c + 1) * FFN_CHUNK)
        gate = _dot(hn, wgate_ref[:, cols])
        up = _dot(hn, wup_ref[:, cols])
        act_ref[:, cols] = (gate * jax.nn.sigmoid(gate) * up).astype(BF16)
    o_ref[...] = x + _dot(act_ref[...], wdown_ref[...])


def _ffn_call(x, g2, w_ffn_in_b, w_ffn_out_b, layer):
    s, d = x.shape
    hidden = w_ffn_out_b.shape[1]
    assert hidden % FFN_CHUNK == 0
    ts = ROW_TILE
    row = lambda t: (t, 0)
    in_specs = [
        pl.BlockSpec((ts, d), row),
        pl.BlockSpec((1, d), lambda t: (0, 0)),
        pl.BlockSpec((None, d, hidden), lambda t: (layer, 0, 0)),
        pl.BlockSpec((None, d, hidden), lambda t: (layer, 0, 1)),
        pl.BlockSpec((None, hidden, d), lambda t: (layer, 0, 0)),
    ]
    blocks = 2 * _nbytes((ts, d), F32) + 3 * _nbytes((d, hidden), BF16)
    scratch_bytes = _nbytes((ts, hidden), BF16)
    temps = _nbytes((ts, d), F32) + _nbytes((ts, d), BF16) + 8 * _nbytes((ts, FFN_CHUNK), F32)
    return pl.pallas_call(
        _ffn_kernel,
        grid=(s // ts,),
        in_specs=in_specs,
        out_specs=pl.BlockSpec((ts, d), row),
        out_shape=jax.ShapeDtypeStruct((s, d), F32),
        scratch_shapes=[pltpu.VMEM((ts, hidden), BF16)],
        compiler_params=pltpu.CompilerParams(
            dimension_semantics=("arbitrary",),
            vmem_limit_bytes=_vmem_limit(blocks, scratch_bytes, temps)),
        name="swiglu_ffn",
    )(x, g2, w_ffn_in_b, w_ffn_in_b, w_ffn_out_b)


def _rotary_tables(seq):
    half = ROT_DIM // 2
    pos = jnp.arange(seq, dtype=F32)
    inv_freq = ROPE_THETA ** (-jnp.arange(0, ROT_DIM, 2, dtype=F32) / ROT_DIM)
    ang = pos[:, None] * inv_freq[None, :]
    cos, sin = jnp.cos(ang), jnp.sin(ang)
    zeros = jnp.zeros((seq, QK_DIM - ROT_DIM), F32)
    z_half = jnp.zeros((seq, half), F32)
    cos_t = jnp.concatenate([cos, cos, jnp.ones_like(zeros)], axis=-1)
    sin_lo = jnp.concatenate([-sin, z_half, zeros], axis=-1)
    sin_hi = jnp.concatenate([z_half, sin, zeros], axis=-1)
    return cos_t, sin_lo, sin_hi


def kernel(x, norm1_g, w_in, q_norm_g, k_norm_g, lam_q1, lam_k1, lam_q2, lam_k2, subln_g,
           pool_w, pool_scale, w_out, norm2_g, w_ffn_in, w_ffn_out):
    b, s, d = x.shape
    assert b == 1 and s % ROW_TILE == 0 and ROW_TILE == ATTN_TILE and ATTN_TILE % CHUNK == 0
    depth = w_in.shape[0]
    tabs = _rotary_tables(s)
    w_in_b = w_in.astype(BF16)
    wvt_b = jnp.swapaxes(w_in[:, :, 3 * d:4 * d], 1, 2).astype(BF16)
    pool_w_b = pool_w.astype(BF16)
    w_out_b = w_out.astype(BF16)
    w_ffn_in_b = w_ffn_in.astype(BF16)
    w_ffn_out_b = w_ffn_out.astype(BF16)
    vec = lambda a, i: a[i][None, :]

    h = x[0]
    for i in range(depth):
        lam_init = 0.8 - 0.6 * math.exp(-0.3 * i)
        q, k, vt = _qkv_call(h, vec(norm1_g, i), w_in_b, wvt_b, vec(q_norm_g, i),
                             vec(k_norm_g, i), tabs, i)
        attn = _attn_call(q, k, vt, vec(subln_g, i), vec(lam_q1, i), vec(lam_k1, i),
                          vec(lam_q2, i), vec(lam_k2, i), lam_init)
        h = _mix_call(h, attn, vec(norm1_g, i), w_in_b, pool_w_b, vec(pool_scale, i),
                      w_out_b, i)
        h = _ffn_call(h, vec(norm2_g, i), w_ffn_in_b, w_ffn_out_b, i)
    return h[None]
```

```python
import functools
import math

import jax
import jax.numpy as jnp
from jax import lax
from jax.experimental import pallas as pl
from jax.experimental.pallas import tpu as pltpu

CHUNK = 64
POOL_WINDOWS = (2, 4, 8, 16)
QK_DIM = 128
V_DIM = 2 * QK_DIM
ROT_DIM = QK_DIM // 4
ROPE_THETA = 500000.0
NORM_EPS = 1e-6
MASK_VALUE = -1e30

V7X_LANES = 128
V7X_VMEM_BYTES = 64 * 1024 * 1024
V7X_VMEM_RESERVE_BYTES = 8 * 1024 * 1024

ROW_TILE = 512
ATTN_TILE = 512
POOL_HALO = max(POOL_WINDOWS)

BF16 = jnp.bfloat16
F32 = jnp.float32
NT_DIMS = (((1,), (1,)), ((), ()))


def _vmem_limit(block_bytes, scratch_bytes, temp_bytes):
    need = 2 * block_bytes + scratch_bytes + temp_bytes
    return int(min(need, V7X_VMEM_BYTES - V7X_VMEM_RESERVE_BYTES))


def _nbytes(shape, dtype):
    return math.prod(shape) * jnp.dtype(dtype).itemsize


def _rms(x):
    return x * lax.rsqrt(jnp.mean(x * x, axis=-1, keepdims=True) + NORM_EPS)


def _dot(a, b):
    return jnp.dot(a, b, preferred_element_type=F32)


def _dot_nt(a, b):
    return lax.dot_general(a, b, NT_DIMS, preferred_element_type=F32)


def _qkv_kernel(x_ref, g1_ref, wq_ref, wk_ref, wvt_ref, qg_ref, kg_ref,
                cos_ref, sinlo_ref, sinhi_ref, q_ref, k_ref, vt_ref, *, q_scale):
    xn = (_rms(x_ref[...]) * g1_ref[...]).astype(BF16)
    cos = cos_ref[...]
    sin_lo = sinlo_ref[...]
    sin_hi = sinhi_ref[...]
    gains = (qg_ref[...] * q_scale, kg_ref[...])

    def norm_rot(t, gain):
        y = _rms(t) * gain
        return (y * cos + pltpu.roll(y, QK_DIM - ROT_DIM // 2, 1) * sin_lo
                + pltpu.roll(y, ROT_DIM // 2, 1) * sin_hi)

    n_heads = wq_ref.shape[1] // V_DIM
    for w_ref, o_ref, gain in ((wq_ref, q_ref, gains[0]), (wk_ref, k_ref, gains[1])):
        for h in range(n_heads):
            cols = slice(h * V_DIM, (h + 1) * V_DIM)
            t = _dot(xn, w_ref[:, cols])
            for c in range(2):
                sub = slice(c * QK_DIM, (c + 1) * QK_DIM)
                o_ref[:, h * V_DIM + c * QK_DIM:h * V_DIM + (c + 1) * QK_DIM] = (
                    norm_rot(t[:, sub], gain).astype(BF16))
    vt_ref[...] = _dot_nt(wvt_ref[...], xn).astype(BF16)


def _qkv_call(x, g1, w_in_b, wvt_b, qg, kg, tabs, layer):
    s, d = x.shape
    ts = ROW_TILE
    row = lambda t: (t, 0)
    const = lambda t: (0, 0)
    wspec = lambda j: pl.BlockSpec((None, d, d), lambda t: (layer, 0, j))
    in_specs = [
        pl.BlockSpec((ts, d), row),
        pl.BlockSpec((1, d), const),
        wspec(1), wspec(2),
        pl.BlockSpec((None, d, d), lambda t: (layer, 0, 0)),
        pl.BlockSpec((1, QK_DIM), const),
        pl.BlockSpec((1, QK_DIM), const),
        pl.BlockSpec((ts, QK_DIM), row),
        pl.BlockSpec((ts, QK_DIM), row),
        pl.BlockSpec((ts, QK_DIM), row),
    ]
    out_specs = [pl.BlockSpec((ts, d), row), pl.BlockSpec((ts, d), row),
                 pl.BlockSpec((None, d, ts), lambda t: (t, 0, 0))]
    blocks = (_nbytes((ts, d), F32) + 3 * _nbytes((d, d), BF16)
              + 3 * _nbytes((ts, QK_DIM), F32) + 3 * _nbytes((ts, d), BF16))
    temps = 2 * _nbytes((ts, d), F32) + _nbytes((ts, d), BF16) + 8 * _nbytes((ts, V_DIM), F32)
    q_scale = (QK_DIM ** -0.5) * math.log2(math.e)
    return pl.pallas_call(
        functools.partial(_qkv_kernel, q_scale=q_scale),
        grid=(s // ts,),
        in_specs=in_specs,
        out_specs=out_specs,
        out_shape=[jax.ShapeDtypeStruct((s, d), BF16), jax.ShapeDtypeStruct((s, d), BF16),
                   jax.ShapeDtypeStruct((s // ts, d, ts), BF16)],
        compiler_params=pltpu.CompilerParams(
            dimension_semantics=("arbitrary",),
            vmem_limit_bytes=_vmem_limit(blocks, 0, temps)),
        name="qkv_proj",
    )(x, g1, w_in_b, w_in_b, wvt_b, qg, kg, *tabs)


SLAB = 16
UNIT = 256
N_UNITS = 2 * (ATTN_TILE // UNIT)
LOOP_UNROLL = 4


def _attn_kernel(q_ref, k_ref, vt_ref, sg_ref, lq1_ref, lk1_ref, lq2_ref, lk2_ref,
                 o_ref, *scratch, lam_init):
    n = N_UNITS
    s_bufs = (scratch[0:n], scratch[n:2 * n])
    top_bufs = (scratch[2 * n:3 * n], scratch[3 * n:4 * n])
    m_refs, l_refs = scratch[4 * n:5 * n], scratch[5 * n:6 * n]
    acc_refs = scratch[6 * n:7 * n]
    i = pl.program_id(1)
    tq = q_ref.shape[0]
    tk = vt_ref.shape[2]
    groups = tq // UNIT

    for u in range(n):
        m_refs[u][...] = jnp.full(m_refs[u].shape, MASK_VALUE, F32)
        l_refs[u][...] = jnp.zeros(l_refs[u].shape, F32)
        acc_refs[u][...] = jnp.zeros(acc_refs[u].shape, F32)

    def scores(blk, slot, u, rows):
        c, g = divmod(u, groups)
        start = pl.multiple_of(blk * tk, tk)
        sub = slice(c * QK_DIM, (c + 1) * QK_DIM)
        s = _dot_nt(k_ref[pl.ds(start, rows), sub], q_ref[g * UNIT:(g + 1) * UNIT, sub])
        s_bufs[slot][u][0:rows, :] = s
        top_bufs[slot][u][...] = jnp.max(s, axis=0, keepdims=True)

    def attend(blk, slot, u, rows, masked):
        g = u % groups

        def slab(r):
            s = s_bufs[slot][u][r * SLAB:(r + 1) * SLAB, :]
            if masked:
                kc = (r * SLAB + lax.broadcasted_iota(jnp.int32, (SLAB, UNIT), 0)) // CHUNK
                qc = (g * UNIT + lax.broadcasted_iota(jnp.int32, (SLAB, UNIT), 1)) // CHUNK
                s = jnp.where(kc <= qc, s, MASK_VALUE)
            return s

        n_slabs = rows // SLAB
        if masked:
            top = functools.reduce(jnp.maximum, [slab(r) for r in range(n_slabs)])
            top = jnp.max(top, axis=0, keepdims=True)
        else:
            top = top_bufs[slot][u][...]
        m_prev = m_refs[u][...]
        m_new = jnp.maximum(m_prev, top)
        alpha = jnp.exp2(m_prev - m_new)
        m_rows = jnp.broadcast_to(m_new, (SLAB, UNIT))
        total = None
        parts = []
        for r in range(n_slabs):
            p = jnp.exp2(slab(r) - m_rows)
            parts.append(p.astype(BF16))
            total = p if total is None else total + p
        l_refs[u][...] = alpha * l_refs[u][...] + jnp.sum(total, axis=0, keepdims=True)
        m_refs[u][...] = m_new
        pt = jnp.concatenate(parts, axis=0)
        acc_refs[u][...] = acc_refs[u][...] * alpha + _dot(vt_ref[blk, :, 0:rows], pt)

    diag_rows = [(u % groups + 1) * UNIT for u in range(n)]
    for u in range(n):
        scores(i, 0, u, diag_rows[u])
    for u in range(n):
        scores(0, 1, u, tk)
        attend(i, 0, u, diag_rows[u], masked=True)

    def run(r, count):
        for step in range(count):
            slot = (1 + step) % 2
            for u in range(n):
                scores(r + step, 1 - slot, u, tk)
                attend(r + step - 1, slot, u, tk, masked=False)

    def body(t, carry):
        run(1 + LOOP_UNROLL * t, LOOP_UNROLL)
        return carry

    lax.fori_loop(0, i // LOOP_UNROLL, body, 0)
    width = LOOP_UNROLL // 2
    while width >= 1:
        def tail(width=width):
            run(1 + 2 * width * (i // (2 * width)), width)
        pl.when(i % (2 * width) >= width)(tail)
        width //= 2

    lam = (jnp.exp(jnp.sum(lq1_ref[...] * lk1_ref[...], axis=-1, keepdims=True))
           - jnp.exp(jnp.sum(lq2_ref[...] * lk2_ref[...], axis=-1, keepdims=True))
           + lam_init)
    for g in range(groups):
        ot = (acc_refs[g][...] / l_refs[g][...]
              - lam * (acc_refs[groups + g][...] / l_refs[groups + g][...]))
        ot = ot * lax.rsqrt(jnp.mean(ot * ot, axis=0, keepdims=True) + NORM_EPS)
        o_ref[g * UNIT:(g + 1) * UNIT, :] = (ot.T * sg_ref[...] * (1.0 - lam_init)).astype(BF16)


def _attn_call(q, k, vt, sg, lq1, lk1, lq2, lk2, lam_init):
    s, d = q.shape
    n_heads = d // V_DIM
    tq = ATTN_TILE
    n_kv, _, tk = vt.shape
    assert tk == tq and N_UNITS == 2 * (tq // UNIT) and LOOP_UNROLL & (LOOP_UNROLL - 1) == 0
    const = lambda h, i: (0, 0)
    in_specs = [
        pl.BlockSpec((tq, V_DIM), lambda h, i: (i, h)),
        pl.BlockSpec((s, V_DIM), lambda h, i: (0, h)),
        pl.BlockSpec((n_kv, V_DIM, tk), lambda h, i: (0, h, 0)),
        pl.BlockSpec((1, V_DIM), const),
        pl.BlockSpec((1, QK_DIM), const),
        pl.BlockSpec((1, QK_DIM), const),
        pl.BlockSpec((1, QK_DIM), const),
        pl.BlockSpec((1, QK_DIM), const),
    ]
    stat = pltpu.VMEM((1, UNIT), F32)
    scratch = (
        [pltpu.VMEM((tk, UNIT), F32)] * (2 * N_UNITS)
        + [stat] * (2 * N_UNITS)
        + [stat] * (2 * N_UNITS)
        + [pltpu.VMEM((V_DIM, UNIT), F32)] * N_UNITS
    )
    blocks = 2 * _nbytes((tq, V_DIM), BF16) + 2 * _nbytes((s, V_DIM), BF16)
    scratch_bytes = (2 * N_UNITS * _nbytes((tk, UNIT), F32)
                     + N_UNITS * _nbytes((V_DIM, UNIT), F32)
                     + 4 * N_UNITS * _nbytes((8, UNIT), F32))
    temps = 4 * _nbytes((tk, tq), F32)
    return pl.pallas_call(
        functools.partial(_attn_kernel, lam_init=lam_init),
        grid=(n_heads, s // tq),
        in_specs=in_specs,
        out_specs=pl.BlockSpec((tq, V_DIM), lambda h, i: (i, h)),
        out_shape=jax.ShapeDtypeStruct((s, d), BF16),
        scratch_shapes=scratch,
        compiler_params=pltpu.CompilerParams(
            dimension_semantics=("arbitrary", "arbitrary"),
            vmem_limit_bytes=_vmem_limit(blocks, scratch_bytes, temps)),
        name="diff_attn",
    )(q, k, vt, sg, lq1, lk1, lq2, lk2)


def _mix_kernel(x_ref, a_ref, g1_ref, wu_ref, wg_ref, pw_ref, ps_ref, wo_ref,
                o_ref, ubuf_ref, merged_ref):
    t = pl.program_id(0)
    ts, d = x_ref.shape
    group = d // len(POOL_WINDOWS)

    @pl.when(t == 0)
    def _():
        ubuf_ref[0:POOL_HALO, :] = jnp.zeros((POOL_HALO, d), F32)

    x = x_ref[...]
    xn = (_rms(x) * g1_ref[...]).astype(BF16)
    ubuf_ref[POOL_HALO:, :] = _dot(xn, wu_ref[...])

    row = t * ts + lax.broadcasted_iota(jnp.int32, (ts, 1), 0)
    for gi, w in enumerate(POOL_WINDOWS):
        cols = slice(gi * group, (gi + 1) * group)
        u = ubuf_ref[POOL_HALO:, cols]
        wsum = u
        for back in range(1, w):
            wsum = wsum + ubuf_ref[POOL_HALO - back:POOL_HALO - back + ts, cols]
        count = jnp.minimum(row + 1, w).astype(F32)
        pooled = wsum / count - u
        pool_out = _dot(pooled.astype(BF16), pw_ref[gi]) * ps_ref[:, cols]
        gate_a = jax.nn.sigmoid(_dot(xn, wg_ref[:, cols]))
        gate_b = jax.nn.sigmoid(_dot(xn, wg_ref[:, d + gi * group:d + (gi + 1) * group]))
        merged_ref[:, cols] = (gate_a * pool_out
                               + gate_b * a_ref[:, cols].astype(F32)).astype(BF16)

    tail = ubuf_ref[ts:ts + POOL_HALO, :]
    ubuf_ref[0:POOL_HALO, :] = tail
    o_ref[...] = x + _dot(merged_ref[...], wo_ref[...])


def _mix_call(x, attn, g1, w_in_b, pool_w_b, pool_scale, w_out_b, layer):
    s, d = x.shape
    ts = ROW_TILE
    n_groups = len(POOL_WINDOWS)
    group = d // n_groups
    row = lambda t: (t, 0)
    const = lambda t: (0, 0)
    in_specs = [
        pl.BlockSpec((ts, d), row),
        pl.BlockSpec((ts, d), row),
        pl.BlockSpec((1, d), const),
        pl.BlockSpec((None, d, d), lambda t: (layer, 0, 0)),
        pl.BlockSpec((None, d, 2 * d), lambda t: (layer, 0, 2)),
        pl.BlockSpec((None, n_groups, group, group), lambda t: (layer, 0, 0, 0)),
        pl.BlockSpec((1, d), const),
        pl.BlockSpec((None, d, d), lambda t: (layer, 0, 0)),
    ]
    blocks = (2 * _nbytes((ts, d), F32) + _nbytes((ts, d), BF16) + 4 * _nbytes((d, d), BF16)
              + _nbytes((n_groups, group, group), BF16))
    scratch_bytes = _nbytes((ts + POOL_HALO, d), F32) + _nbytes((ts, d), BF16)
    temps = 2 * _nbytes((ts, d), F32) + 8 * _nbytes((ts, group), F32)
    return pl.pallas_call(
        _mix_kernel,
        grid=(s // ts,),
        in_specs=in_specs,
        out_specs=pl.BlockSpec((ts, d), row),
        out_shape=jax.ShapeDtypeStruct((s, d), F32),
        scratch_shapes=[pltpu.VMEM((ts + POOL_HALO, d), F32), pltpu.VMEM((ts, d), BF16)],
        compiler_params=pltpu.CompilerParams(
            dimension_semantics=("arbitrary",),
            vmem_limit_bytes=_vmem_limit(blocks, scratch_bytes, temps)),
        name="pool_gate_out",
    )(x, attn, g1, w_in_b, w_in_b, pool_w_b, pool_scale, w_out_b)


FFN_CHUNK = 256


def _ffn_kernel(x_ref, g2_ref, wgate_ref, wup_ref, wdown_ref, o_ref, act_ref):
    x = x_ref[...]
    hn = (_rms(x) * g2_ref[...]).astype(BF16)
    hidden = wgate_ref.shape[1]
    for c in range(hidden // FFN_CHUNK):
        cols = slice(c * FFN_CHUNK, (c + 1) * FFN_CHUNK)
        gate = _dot(hn, wgate_ref[:, cols])
        up = _dot(hn, wup_ref[:, cols])
        act_ref[:, cols] = (gate * jax.nn.sigmoid(gate) * up).astype(BF16)
    o_ref[...] = x + _dot(act_ref[...], wdown_ref[...])


def _ffn_call(x, g2, w_ffn_in_b, w_ffn_out_b, layer):
    s, d = x.shape
    hidden = w_ffn_out_b.shape[1]
    assert hidden % FFN_CHUNK == 0
    ts = ROW_TILE
    row = lambda t: (t, 0)
    in_specs = [
        pl.BlockSpec((ts, d), row),
        pl.BlockSpec((1, d), lambda t: (0, 0)),
        pl.BlockSpec((None, d, hidden), lambda t: (layer, 0, 0)),
        pl.BlockSpec((None, d, hidden), lambda t: (layer, 0, 1)),
        pl.BlockSpec((None, hidden, d), lambda t: (layer, 0, 0)),
    ]
    blocks = 2 * _nbytes((ts, d), F32) + 3 * _nbytes((d, hidden), BF16)
    scratch_bytes = _nbytes((ts, hidden), BF16)
    temps = _nbytes((ts, d), F32) + _nbytes((ts, d), BF16) + 8 * _nbytes((ts, FFN_CHUNK), F32)
    return pl.pallas_call(
        _ffn_kernel,
        grid=(s // ts,),
        in_specs=in_specs,
        out_specs=pl.BlockSpec((ts, d), row),
        out_shape=jax.ShapeDtypeStruct((s, d), F32),
        scratch_shapes=[pltpu.VMEM((ts, hidden), BF16)],
        compiler_params=pltpu.CompilerParams(
            dimension_semantics=("arbitrary",),
            vmem_limit_bytes=_vmem_limit(blocks, scratch_bytes, temps)),
        name="swiglu_ffn",
    )(x, g2, w_ffn_in_b, w_ffn_in_b, w_ffn_out_b)


def _rotary_tables(seq):
    half = ROT_DIM // 2
    pos = jnp.arange(seq, dtype=F32)
    inv_freq = ROPE_THETA ** (-jnp.arange(0, ROT_DIM, 2, dtype=F32) / ROT_DIM)
    ang = pos[:, None] * inv_freq[None, :]
    cos, sin = jnp.cos(ang), jnp.sin(ang)
    zeros = jnp.zeros((seq, QK_DIM - ROT_DIM), F32)
    z_half = jnp.zeros((seq, half), F32)
    cos_t = jnp.concatenate([cos, cos, jnp.ones_like(zeros)], axis=-1)
    sin_lo = jnp.concatenate([-sin, z_half, zeros], axis=-1)
    sin_hi = jnp.concatenate([z_half, sin, zeros], axis=-1)
    return cos_t, sin_lo, sin_hi


def kernel(x, norm1_g, w_in, q_norm_g, k_norm_g, lam_q1, lam_k1, lam_q2, lam_k2, subln_g,
           pool_w, pool_scale, w_out, norm2_g, w_ffn_in, w_ffn_out):
    b, s, d = x.shape
    assert b == 1 and s % ROW_TILE == 0 and ROW_TILE == ATTN_TILE and ATTN_TILE % CHUNK == 0
    depth = w_in.shape[0]
    tabs = _rotary_tables(s)
    w_in_b = w_in.astype(BF16)
    wvt_b = jnp.swapaxes(w_in[:, :, 3 * d:4 * d], 1, 2).astype(BF16)
    pool_w_b = pool_w.astype(BF16)
    w_out_b = w_out.astype(BF16)
    w_ffn_in_b = w_ffn_in.astype(BF16)
    w_ffn_out_b = w_ffn_out.astype(BF16)
    vec = lambda a, i: a[i][None, :]

    h = x[0]
    for i in range(depth):
        lam_init = 0.8 - 0.6 * math.exp(-0.3 * i)
        q, k, vt = _qkv_call(h, vec(norm1_g, i), w_in_b, wvt_b, vec(q_norm_g, i),
                             vec(k_norm_g, i), tabs, i)
        attn = _attn_call(q, k, vt, vec(subln_g, i), vec(lam_q1, i), vec(lam_k1, i),
                          vec(lam_q2, i), vec(lam_k2, i), lam_init)
        h = _mix_call(h, attn, vec(norm1_g, i), w_in_b, pool_w_b, vec(pool_scale, i),
                      w_out_b, i)
        h = _ffn_call(h, vec(norm2_g, i), w_ffn_in_b, w_ffn_out_b, i)
    return h[None]
```

```python
import functools
import math

import jax
import jax.numpy as jnp
from jax import lax
from jax.experimental import pallas as pl
from jax.experimental.pallas import tpu as pltpu

CHUNK = 64
POOL_WINDOWS = (2, 4, 8, 16)
QK_DIM = 128
V_DIM = 2 * QK_DIM
ROT_DIM = QK_DIM // 4
ROPE_THETA = 500000.0
NORM_EPS = 1e-6
MASK_VALUE = -1e30

V7X_LANES = 128
V7X_VMEM_BYTES = 64 * 1024 * 1024
V7X_VMEM_RESERVE_BYTES = 8 * 1024 * 1024

ROW_TILE = 512
ATTN_TILE = 512
POOL_HALO = max(POOL_WINDOWS)

BF16 = jnp.bfloat16
F32 = jnp.float32
NT_DIMS = (((1,), (1,)), ((), ()))


def _vmem_limit(block_bytes, scratch_bytes, temp_bytes):
    need = 2 * block_bytes + scratch_bytes + temp_bytes
    return int(min(need, V7X_VMEM_BYTES - V7X_VMEM_RESERVE_BYTES))


def _nbytes(shape, dtype):
    return math.prod(shape) * jnp.dtype(dtype).itemsize


def _rms(x):
    return x * lax.rsqrt(jnp.mean(x * x, axis=-1, keepdims=True) + NORM_EPS)


def _dot(a, b):
    return jnp.dot(a, b, preferred_element_type=F32)


def _dot_nt(a, b):
    return lax.dot_general(a, b, NT_DIMS, preferred_element_type=F32)


def _qkv_kernel(x_ref, g1_ref, wqk_ref, wvt_ref, gain_ref, cos_ref, sinlo_ref, sinhi_ref,
                qk_ref, vt_ref, xn_ref, raw0_ref, raw1_ref, *, q_scale):
    n_chunks = wqk_ref.shape[0]
    d_v = wvt_ref.shape[0]
    raws = (raw0_ref, raw1_ref)
    xn_ref[...] = (_rms(x_ref[...]) * g1_ref[...]).astype(BF16)

    def project(j, slot):
        raws[slot][...] = _dot(xn_ref[...], wqk_ref[j])

    def finish(j, slot):
        is_q = j < n_chunks // 2
        gain = gain_ref[j // (n_chunks // 2)] * jnp.where(is_q, q_scale, 1.0)
        for c in range(2):
            sub = slice(c * QK_DIM, (c + 1) * QK_DIM)
            y = _rms(raws[slot][:, sub]) * gain
            y = (y * cos_ref[...] + pltpu.roll(y, QK_DIM - ROT_DIM // 2, 1) * sinlo_ref[...]
                 + pltpu.roll(y, ROT_DIM // 2, 1) * sinhi_ref[...])
            qk_ref[j, :, sub] = y.astype(BF16)

    project(0, 0)
    rows_v = 2 * d_v // n_chunks

    def body(t, carry):
        j = 2 * t
        project(j + 1, 1)
        finish(j, 0)
        r0 = pl.multiple_of(t * rows_v, rows_v)
        vt_ref[pl.ds(r0, rows_v), :] = _dot_nt(wvt_ref[pl.ds(r0, rows_v), :],
                                               xn_ref[...]).astype(BF16)
        project(jnp.minimum(j + 2, n_chunks - 1), 0)
        finish(j + 1, 1)
        return carry

    lax.fori_loop(0, n_chunks // 2, body, 0)


def _qkv_call(x, g1, wqk_b, wvt_b, gains, tabs, layer):
    s, d = x.shape
    ts = ROW_TILE
    n_chunks = wqk_b.shape[1]
    row = lambda t: (t, 0)
    const = lambda t: (0, 0)
    in_specs = [
        pl.BlockSpec((ts, d), row),
        pl.BlockSpec((1, d), const),
        pl.BlockSpec((None, n_chunks, d, V_DIM), lambda t: (layer, 0, 0, 0)),
        pl.BlockSpec((None, d, d), lambda t: (layer, 0, 0)),
        pl.BlockSpec((2, 1, QK_DIM), lambda t: (0, 0, 0)),
        pl.BlockSpec((ts, QK_DIM), row),
        pl.BlockSpec((ts, QK_DIM), row),
        pl.BlockSpec((ts, QK_DIM), row),
    ]
    out_specs = [pl.BlockSpec((n_chunks, ts, V_DIM), lambda t: (0, t, 0)),
                 pl.BlockSpec((None, d, ts), lambda t: (t, 0, 0))]
    blocks = (_nbytes((ts, d), F32) + 3 * _nbytes((d, d), BF16)
              + 3 * _nbytes((ts, QK_DIM), F32) + 3 * _nbytes((ts, d), BF16))
    scratch_bytes = _nbytes((ts, d), BF16) + 2 * _nbytes((ts, V_DIM), F32)
    temps = _nbytes((ts, d), F32) + 4 * _nbytes((ts, V_DIM), F32)
    q_scale = (QK_DIM ** -0.5) * math.log2(math.e)
    return pl.pallas_call(
        functools.partial(_qkv_kernel, q_scale=q_scale),
        grid=(s // ts,),
        in_specs=in_specs,
        out_specs=out_specs,
        out_shape=[jax.ShapeDtypeStruct((n_chunks, s, V_DIM), BF16),
                   jax.ShapeDtypeStruct((s // ts, d, ts), BF16)],
        scratch_shapes=[pltpu.VMEM((ts, d), BF16), pltpu.VMEM((ts, V_DIM), F32),
                        pltpu.VMEM((ts, V_DIM), F32)],
        compiler_params=pltpu.CompilerParams(
            dimension_semantics=("arbitrary",),
            vmem_limit_bytes=_vmem_limit(blocks, scratch_bytes, temps)),
        name="qkv_proj",
    )(x, g1, wqk_b, wvt_b, gains, *tabs)


SLAB = 16
UNIT = 256
N_UNITS = 2 * (ATTN_TILE // UNIT)
LOOP_UNROLL = 4


def _attn_kernel(q_ref, k_ref, vt_ref, sg_ref, lq1_ref, lk1_ref, lq2_ref, lk2_ref,
                 o_ref, *scratch, lam_init):
    n = N_UNITS
    s_bufs = (scratch[0:n], scratch[n:2 * n])
    top_bufs = (scratch[2 * n:3 * n], scratch[3 * n:4 * n])
    m_refs, l_refs = scratch[4 * n:5 * n], scratch[5 * n:6 * n]
    acc_refs = scratch[6 * n:7 * n]
    i = pl.program_id(1)
    tq = q_ref.shape[0]
    tk = vt_ref.shape[2]
    groups = tq // UNIT

    for u in range(n):
        m_refs[u][...] = jnp.full(m_refs[u].shape, MASK_VALUE, F32)
        l_refs[u][...] = jnp.zeros(l_refs[u].shape, F32)
        acc_refs[u][...] = jnp.zeros(acc_refs[u].shape, F32)

    def scores(blk, slot, u, rows):
        c, g = divmod(u, groups)
        start = pl.multiple_of(blk * tk, tk)
        sub = slice(c * QK_DIM, (c + 1) * QK_DIM)
        s = _dot_nt(k_ref[pl.ds(start, rows), sub], q_ref[g * UNIT:(g + 1) * UNIT, sub])
        s_bufs[slot][u][0:rows, :] = s
        top_bufs[slot][u][...] = jnp.max(s, axis=0, keepdims=True)

    def attend(blk, slot, u, rows, masked):
        g = u % groups

        def slab(r):
            s = s_bufs[slot][u][r * SLAB:(r + 1) * SLAB, :]
            if masked:
                kc = (r * SLAB + lax.broadcasted_iota(jnp.int32, (SLAB, UNIT), 0)) // CHUNK
                qc = (g * UNIT + lax.broadcasted_iota(jnp.int32, (SLAB, UNIT), 1)) // CHUNK
                s = jnp.where(kc <= qc, s, MASK_VALUE)
            return s

        n_slabs = rows // SLAB
        if masked:
            top = functools.reduce(jnp.maximum, [slab(r) for r in range(n_slabs)])
            top = jnp.max(top, axis=0, keepdims=True)
        else:
            top = top_bufs[slot][u][...]
        m_prev = m_refs[u][...]
        m_new = jnp.maximum(m_prev, top)
        alpha = jnp.exp2(m_prev - m_new)
        m_rows = jnp.broadcast_to(m_new, (SLAB, UNIT))
        total = None
        parts = []
        for r in range(n_slabs):
            p = jnp.exp2(slab(r) - m_rows)
            parts.append(p.astype(BF16))
            total = p if total is None else total + p
        l_refs[u][...] = alpha * l_refs[u][...] + jnp.sum(total, axis=0, keepdims=True)
        m_refs[u][...] = m_new
        pt = jnp.concatenate(parts, axis=0)
        acc_refs[u][...] = acc_refs[u][...] * alpha + _dot(vt_ref[blk, :, 0:rows], pt)

    diag_rows = [(u % groups + 1) * UNIT for u in range(n)]
    for u in range(n):
        scores(i, 0, u, diag_rows[u])
    for u in range(n):
        scores(0, 1, u, tk)
        attend(i, 0, u, diag_rows[u], masked=True)

    def run(r, count):
        for step in range(count):
            slot = (1 + step) % 2
            for u in range(n):
                scores(r + step, 1 - slot, u, tk)
                attend(r + step - 1, slot, u, tk, masked=False)

    def body(t, carry):
        run(1 + LOOP_UNROLL * t, LOOP_UNROLL)
        return carry

    lax.fori_loop(0, i // LOOP_UNROLL, body, 0)
    width = LOOP_UNROLL // 2
    while width >= 1:
        def tail(width=width):
            run(1 + 2 * width * (i // (2 * width)), width)
        pl.when(i % (2 * width) >= width)(tail)
        width //= 2

    lam = (jnp.exp(jnp.sum(lq1_ref[...] * lk1_ref[...], axis=-1, keepdims=True))
           - jnp.exp(jnp.sum(lq2_ref[...] * lk2_ref[...], axis=-1, keepdims=True))
           + lam_init)
    for g in range(groups):
        ot = (acc_refs[g][...] / l_refs[g][...]
              - lam * (acc_refs[groups + g][...] / l_refs[groups + g][...]))
        ot = ot * lax.rsqrt(jnp.mean(ot * ot, axis=0, keepdims=True) + NORM_EPS)
        o_ref[g * UNIT:(g + 1) * UNIT, :] = (ot.T * sg_ref[...] * (1.0 - lam_init)).astype(BF16)


def _attn_call(qk, vt, sg, lq1, lk1, lq2, lk2, lam_init):
    n_heads, s = qk.shape[0] // 2, qk.shape[1]
    d = n_heads * V_DIM
    tq = ATTN_TILE
    n_kv, _, tk = vt.shape
    assert tk == tq and N_UNITS == 2 * (tq // UNIT) and LOOP_UNROLL & (LOOP_UNROLL - 1) == 0
    const = lambda h, i: (0, 0)
    in_specs = [
        pl.BlockSpec((None, tq, V_DIM), lambda h, i: (h, i, 0)),
        pl.BlockSpec((None, s, V_DIM), lambda h, i: (n_heads + h, 0, 0)),
        pl.BlockSpec((n_kv, V_DIM, tk), lambda h, i: (0, h, 0)),
        pl.BlockSpec((1, V_DIM), const),
        pl.BlockSpec((1, QK_DIM), const),
        pl.BlockSpec((1, QK_DIM), const),
        pl.BlockSpec((1, QK_DIM), const),
        pl.BlockSpec((1, QK_DIM), const),
    ]
    stat = pltpu.VMEM((1, UNIT), F32)
    scratch = (
        [pltpu.VMEM((tk, UNIT), F32)] * (2 * N_UNITS)
        + [stat] * (2 * N_UNITS)
        + [stat] * (2 * N_UNITS)
        + [pltpu.VMEM((V_DIM, UNIT), F32)] * N_UNITS
    )
    blocks = 2 * _nbytes((tq, V_DIM), BF16) + 2 * _nbytes((s, V_DIM), BF16)
    scratch_bytes = (2 * N_UNITS * _nbytes((tk, UNIT), F32)
                     + N_UNITS * _nbytes((V_DIM, UNIT), F32)
                     + 4 * N_UNITS * _nbytes((8, UNIT), F32))
    temps = 4 * _nbytes((tk, tq), F32)
    return pl.pallas_call(
        functools.partial(_attn_kernel, lam_init=lam_init),
        grid=(n_heads, s // tq),
        in_specs=in_specs,
        out_specs=pl.BlockSpec((tq, V_DIM), lambda h, i: (i, h)),
        out_shape=jax.ShapeDtypeStruct((s, d), BF16),
        scratch_shapes=scratch,
        compiler_params=pltpu.CompilerParams(
            dimension_semantics=("arbitrary", "arbitrary"),
            vmem_limit_bytes=_vmem_limit(blocks, scratch_bytes, temps)),
        name="diff_attn",
    )(qk, qk, vt, sg, lq1, lk1, lq2, lk2)


def _mix_kernel(x_ref, a_ref, g1_ref, wu_ref, wg_ref, pw_ref, ps_ref, wo_ref,
                o_ref, ubuf_ref, merged_ref):
    t = pl.program_id(0)
    ts, d = x_ref.shape
    group = d // len(POOL_WINDOWS)

    @pl.when(t == 0)
    def _():
        ubuf_ref[0:POOL_HALO, :] = jnp.zeros((POOL_HALO, d), F32)

    x = x_ref[...]
    xn = (_rms(x) * g1_ref[...]).astype(BF16)
    ubuf_ref[POOL_HALO:, :] = _dot(xn, wu_ref[...])

    row = t * ts + lax.broadcasted_iota(jnp.int32, (ts, 1), 0)
    for gi, w in enumerate(POOL_WINDOWS):
        cols = slice(gi * group, (gi + 1) * group)
        u = ubuf_ref[POOL_HALO:, cols]
        wsum = u
        for back in range(1, w):
            wsum = wsum + ubuf_ref[POOL_HALO - back:POOL_HALO - back + ts, cols]
        count = jnp.minimum(row + 1, w).astype(F32)
        pooled = wsum / count - u
        pool_out = _dot(pooled.astype(BF16), pw_ref[gi]) * ps_ref[:, cols]
        gate_a = jax.nn.sigmoid(_dot(xn, wg_ref[:, cols]))
        gate_b = jax.nn.sigmoid(_dot(xn, wg_ref[:, d + gi * group:d + (gi + 1) * group]))
        merged_ref[:, cols] = (gate_a * pool_out
                               + gate_b * a_ref[:, cols].astype(F32)).astype(BF16)

    tail = ubuf_ref[ts:ts + POOL_HALO, :]
    ubuf_ref[0:POOL_HALO, :] = tail
    o_ref[...] = x + _dot(merged_ref[...], wo_ref[...])


def _mix_call(x, attn, g1, w_in_b, pool_w_b, pool_scale, w_out_b, layer):
    s, d = x.shape
    ts = ROW_TILE
    n_groups = len(POOL_WINDOWS)
    group = d // n_groups
    row = lambda t: (t, 0)
    const = lambda t: (0, 0)
    in_specs = [
        pl.BlockSpec((ts, d), row),
        pl.BlockSpec((ts, d), row),
        pl.BlockSpec((1, d), const),
        pl.BlockSpec((None, d, d), lambda t: (layer, 0, 0)),
        pl.BlockSpec((None, d, 2 * d), lambda t: (layer, 0, 2)),
        pl.BlockSpec((None, n_groups, group, group), lambda t: (layer, 0, 0, 0)),
        pl.BlockSpec((1, d), const),
        pl.BlockSpec((None, d, d), lambda t: (layer, 0, 0)),
    ]
    blocks = (2 * _nbytes((ts, d), F32) + _nbytes((ts, d), BF16) + 4 * _nbytes((d, d), BF16)
              + _nbytes((n_groups, group, group), BF16))
    scratch_bytes = _nbytes((ts + POOL_HALO, d), F32) + _nbytes((ts, d), BF16)
    temps = 2 * _nbytes((ts, d), F32) + 8 * _nbytes((ts, group), F32)
    return pl.pallas_call(
        _mix_kernel,
        grid=(s // ts,),
        in_specs=in_specs,
        out_specs=pl.BlockSpec((ts, d), row),
        out_shape=jax.ShapeDtypeStruct((s, d), F32),
        scratch_shapes=[pltpu.VMEM((ts + POOL_HALO, d), F32), pltpu.VMEM((ts, d), BF16)],
        compiler_params=pltpu.CompilerParams(
            dimension_semantics=("arbitrary",),
            vmem_limit_bytes=_vmem_limit(blocks, scratch_bytes, temps)),
        name="pool_gate_out",
    )(x, attn, g1, w_in_b, w_in_b, pool_w_b, pool_scale, w_out_b)


FFN_CHUNK = 256


def _ffn_kernel(x_ref, g2_ref, wgate_ref, wup_ref, wdown_ref, o_ref, act_ref):
    x = x_ref[...]
    hn = (_rms(x) * g2_ref[...]).astype(BF16)
    hidden = wgate_ref.shape[1]
    for c in range(hidden // FFN_CHUNK):
        cols = slice(c * FFN_CHUNK, (c + 1) * FFN_CHUNK)
        gate = _dot(hn, wgate_ref[:, cols])
        up = _dot(hn, wup_ref[:, cols])
        act_ref[:, cols] = (gate * jax.nn.sigmoid(gate) * up).astype(BF16)
    o_ref[...] = x + _dot(act_ref[...], wdown_ref[...])


def _ffn_call(x, g2, w_ffn_in_b, w_ffn_out_b, layer):
    s, d = x.shape
    hidden = w_ffn_out_b.shape[1]
    assert hidden % FFN_CHUNK == 0
    ts = ROW_TILE
    row = lambda t: (t, 0)
    in_specs = [
        pl.BlockSpec((ts, d), row),
        pl.BlockSpec((1, d), lambda t: (0, 0)),
        pl.BlockSpec((None, d, hidden), lambda t: (layer, 0, 0)),
        pl.BlockSpec((None, d, hidden), lambda t: (layer, 0, 1)),
        pl.BlockSpec((None, hidden, d), lambda t: (layer, 0, 0)),
    ]
    blocks = 2 * _nbytes((ts, d), F32) + 3 * _nbytes((d, hidden), BF16)
    scratch_bytes = _nbytes((ts, hidden), BF16)
    temps = _nbytes((ts, d), F32) + _nbytes((ts, d), BF16) + 8 * _nbytes((ts, FFN_CHUNK), F32)
    return pl.pallas_call(
        _ffn_kernel,
        grid=(s // ts,),
        in_specs=in_specs,
        out_specs=pl.BlockSpec((ts, d), row),
        out_shape=jax.ShapeDtypeStruct((s, d), F32),
        scratch_shapes=[pltpu.VMEM((ts, hidden), BF16)],
        compiler_params=pltpu.CompilerParams(
            dimension_semantics=("arbitrary",),
            vmem_limit_bytes=_vmem_limit(blocks, scratch_bytes, temps)),
        name="swiglu_ffn",
    )(x, g2, w_ffn_in_b, w_ffn_in_b, w_ffn_out_b)


def _rotary_tables(seq):
    half = ROT_DIM // 2
    pos = jnp.arange(seq, dtype=F32)
    inv_freq = ROPE_THETA ** (-jnp.arange(0, ROT_DIM, 2, dtype=F32) / ROT_DIM)
    ang = pos[:, None] * inv_freq[None, :]
    cos, sin = jnp.cos(ang), jnp.sin(ang)
    zeros = jnp.zeros((seq, QK_DIM - ROT_DIM), F32)
    z_half = jnp.zeros((seq, half), F32)
    cos_t = jnp.concatenate([cos, cos, jnp.ones_like(zeros)], axis=-1)
    sin_lo = jnp.concatenate([-sin, z_half, zeros], axis=-1)
    sin_hi = jnp.concatenate([z_half, sin, zeros], axis=-1)
    return cos_t, sin_lo, sin_hi


def kernel(x, norm1_g, w_in, q_norm_g, k_norm_g, lam_q1, lam_k1, lam_q2, lam_k2, subln_g,
           pool_w, pool_scale, w_out, norm2_g, w_ffn_in, w_ffn_out):
    b, s, d = x.shape
    assert b == 1 and s % ROW_TILE == 0 and ROW_TILE == ATTN_TILE and ATTN_TILE % CHUNK == 0
    depth = w_in.shape[0]
    tabs = _rotary_tables(s)
    w_in_b = w_in.astype(BF16)
    wvt_b = jnp.swapaxes(w_in[:, :, 3 * d:4 * d], 1, 2).astype(BF16)
    wqk_b = jnp.swapaxes(w_in_b[:, :, d:3 * d].reshape(depth, d, 2 * d // V_DIM, V_DIM), 1, 2)
    pool_w_b = pool_w.astype(BF16)
    w_out_b = w_out.astype(BF16)
    w_ffn_in_b = w_ffn_in.astype(BF16)
    w_ffn_out_b = w_ffn_out.astype(BF16)
    vec = lambda a, i: a[i][None, :]

    h = x[0]
    for i in range(depth):
        lam_init = 0.8 - 0.6 * math.exp(-0.3 * i)
        gains = jnp.stack([q_norm_g[i], k_norm_g[i]])[:, None, :]
        qk, vt = _qkv_call(h, vec(norm1_g, i), wqk_b, wvt_b, gains, tabs, i)
        attn = _attn_call(qk, vt, vec(subln_g, i), vec(lam_q1, i), vec(lam_k1, i),
                          vec(lam_q2, i), vec(lam_k2, i), lam_init)
        h = _mix_call(h, attn, vec(norm1_g, i), w_in_b, pool_w_b, vec(pool_scale, i),
                      w_out_b, i)
        h = _ffn_call(h, vec(norm2_g, i), w_ffn_in_b, w_ffn_out_b, i)
    return h[None]
```
